```python
import math
import jax, jax.numpy as jnp
from jax import lax
import numpy as np

D_MODEL = 1024
BATCH = 4
SEQ = 4096
DEPTH = 1

MEM_LEN = 256
RMS_EPS = 1e-6
NEG_INF = -1e30
FORCE = 1e6

POOL_WINDOWS = (2, 4, 8, 16)
POOL_GROUPS = len(POOL_WINDOWS)
POOL_GROUP_DIM = D_MODEL // 8
POOL_WIDTH = POOL_GROUPS * POOL_GROUP_DIM

NSA_HEADS = 16
NSA_KV_HEADS = 4
NSA_GROUP = NSA_HEADS // NSA_KV_HEADS
HEAD_DIM = 64
NSA_WIDTH = NSA_HEADS * HEAD_DIM
KV_WIDTH = NSA_KV_HEADS * HEAD_DIM
CMP_LEN = 32
CMP_STRIDE = 16
SEL_BLOCK = 64
N_SELECT = 16
WINDOW = 512
NSA_Q_BLOCK = 64

X_HEADS = 4
X_HEAD_DIM = 128
X_WIDTH = X_HEADS * X_HEAD_DIM

N_BUCKETS = 32
MAX_DISTANCE = 128

PEER_HEADS = 8
N_KEYS = 128
N_EXPERTS = N_KEYS * N_KEYS
PEER_QDIM = 256
PEER_HALF = PEER_QDIM // 2
PEER_TOPK = 16
PEER_TOKEN_BLOCK = 128

N_BRANCHES = 3
IN_WIDTHS = (POOL_WIDTH, NSA_WIDTH, 6 * KV_WIDTH, 3 * NSA_HEADS, X_WIDTH, N_BRANCHES * D_MODEL)
IN_SPLITS = tuple(int(s) for s in np.cumsum(IN_WIDTHS)[:-1])
D_IN = int(sum(IN_WIDTHS))

kernel_name = "hybrid_pool_nsa_mem_peer_block"


def rms_norm(x, g):
    xf = x.astype(jnp.float32)
    y = xf * lax.rsqrt(jnp.mean(xf * xf, axis=-1, keepdims=True) + RMS_EPS)
    return (y * g.astype(jnp.float32)).astype(x.dtype)


def masked_softmax(s, mask):
    return jax.nn.softmax(jnp.where(mask, s, NEG_INF), axis=-1)


def t5_bucket(dist):
    n = jnp.maximum(dist, 0)
    max_exact = N_BUCKETS // 2
    nf = jnp.maximum(n, 1).astype(jnp.float32)
    large = max_exact + (jnp.log(nf / max_exact) / math.log(MAX_DISTANCE / max_exact)
                         * (N_BUCKETS - max_exact)).astype(jnp.int32)
    large = jnp.minimum(large, N_BUCKETS - 1)
    return jnp.where(n < max_exact, n, large)


def pool_mixer(u, w_grp, scale):
    B, T, _ = u.shape
    c = jnp.cumsum(u.astype(jnp.float32), axis=1)
    t_idx = jnp.arange(T)
    outs = []
    for g, w in enumerate(POOL_WINDOWS):
        sl = slice(g * POOL_GROUP_DIM, (g + 1) * POOL_GROUP_DIM)
        cg = c[..., sl]
        shifted = jnp.pad(cg, ((0, 0), (w, 0), (0, 0)))[:, :T]
        cnt = jnp.minimum(t_idx + 1, w).astype(jnp.float32)[None, :, None]
        outs.append((cg - shifted) / cnt - u[..., sl].astype(jnp.float32))
    pooled = jnp.stack(outs, axis=2)
    mixed = jnp.einsum('btgc,gcd->btgd', pooled, w_grp.astype(jnp.float32))
    return (mixed.reshape(B, T, POOL_WIDTH) * scale.astype(jnp.float32)).astype(u.dtype)


def nsa_attention(q, k_cmp_tok, v_cmp_tok, k_sel, v_sel, k_win, v_win,
                  w_cmp_k, w_cmp_v, pe_k, pe_v, rel_bias):
    B, T = q.shape[:2]
    G, R, dh = NSA_KV_HEADS, NSA_GROUP, HEAD_DIM
    QB = NSA_Q_BLOCK
    n_cmp = (T - CMP_LEN) // CMP_STRIDE + 1
    n_sel = T // SEL_BLOCK
    n_top = min(N_SELECT, n_sel)

    cmp_start = jnp.arange(n_cmp) * CMP_STRIDE
    cmp_end = cmp_start + CMP_LEN - 1
    tok_idx = cmp_start[:, None] + jnp.arange(CMP_LEN)[None, :]

    def compress(tok, pe, w):
        blk = tok[:, tok_idx] + pe[None, None, :, None, :]
        blk = blk.transpose(0, 1, 3, 2, 4).reshape(B, n_cmp, G, CMP_LEN * dh)
        return blk @ w

    kc = compress(k_cmp_tok, pe_k, w_cmp_k)
    vc = compress(v_cmp_tok, pe_v, w_cmp_v)

    sel_start = jnp.arange(n_sel) * SEL_BLOCK
    overlap = ((cmp_start[:, None] < sel_start[None, :] + SEL_BLOCK) &
               (cmp_start[:, None] + CMP_LEN > sel_start[None, :])).astype(jnp.float32)

    ks_blk = k_sel.reshape(B, n_sel, SEL_BLOCK, G, dh).transpose(0, 3, 1, 2, 4)
    vs_blk = v_sel.reshape(B, n_sel, SEL_BLOCK, G, dh).transpose(0, 3, 1, 2, 4)
    kw_pad = jnp.pad(k_win, ((0, 0), (WINDOW, 0), (0, 0), (0, 0)))
    vw_pad = jnp.pad(v_win, ((0, 0), (WINDOW, 0), (0, 0), (0, 0)))

    qg = q.reshape(B, T, G, R, dh) * (dh ** -0.5)
    table_gr = rel_bias.astype(jnp.float32).reshape(N_BUCKETS, G, R)
    bi = jnp.arange(B)[:, None, None, None]
    gi = jnp.arange(G)[None, :, None, None]
    gi5 = jnp.arange(G)[None, :, None, None, None]
    sel_ids = jnp.arange(n_sel)
    win_off = jnp.arange(WINDOW + QB)

    def head_bias(dist):
        return table_gr[t5_bucket(dist)].transpose(2, 3, 0, 1)

    def block(t0):
        qb = lax.dynamic_slice_in_dim(qg, t0, QB, axis=1)
        tq = t0 + jnp.arange(QB)

        dist_c = tq[:, None] - cmp_end[None, :]
        mask_c = dist_c >= 0
        s_c = jnp.einsum('bqgrd,bcgd->bgrqc', qb, kc).astype(jnp.float32) + head_bias(dist_c)
        has = jnp.any(mask_c, axis=-1).astype(jnp.float32)
        p_c = masked_softmax(s_c, mask_c) * has[None, None, None, :, None]
        o_c = jnp.einsum('bgrqc,bcgd->bqgrd', p_c.astype(vc.dtype), vc)

        imp = jnp.einsum('bgrqc,cs->bgqs', p_c, overlap)
        blk_q = tq // SEL_BLOCK
        forced = ((sel_ids[None, :] == 0) | (sel_ids[None, :] == blk_q[:, None]) |
                  (sel_ids[None, :] == blk_q[:, None] - 1))
        causal_blk = sel_ids[None, :] <= blk_q[:, None]
        imp = jnp.where(forced, FORCE, jnp.where(causal_blk, imp, -FORCE))
        _, idx = lax.top_k(imp, n_top)

        k_g = ks_blk[bi, gi, idx]
        v_g = vs_blk[bi, gi, idx]
        pos = idx[..., None] * SEL_BLOCK + jnp.arange(SEL_BLOCK)
        dist_s = tq[None, None, :, None, None] - pos
        bias_s = jnp.moveaxis(table_gr[t5_bucket(dist_s), gi5], -1, 2)
        nk = n_top * SEL_BLOCK
        s_s = jnp.einsum('bqgrd,bgqkld->bgrqkl', qb, k_g).astype(jnp.float32) + bias_s
        s_s = s_s.reshape(B, G, R, QB, nk)
        mask_s = (dist_s >= 0).reshape(B, G, 1, QB, nk)
        p_s = masked_softmax(s_s, mask_s)
        o_s = jnp.einsum('bgrqn,bgqnd->bqgrd', p_s.astype(v_g.dtype), v_g.reshape(B, G, QB, nk, dh))

        k_w = lax.dynamic_slice_in_dim(kw_pad, t0, WINDOW + QB, axis=1)
        v_w = lax.dynamic_slice_in_dim(vw_pad, t0, WINDOW + QB, axis=1)
        pos_w = t0 - WINDOW + win_off
        dist_w = tq[:, None] - pos_w[None, :]
        mask_w = (dist_w >= 0) & (dist_w < WINDOW) & (pos_w[None, :] >= 0)
        s_w = jnp.einsum('bqgrd,bkgd->bgrqk', qb, k_w).astype(jnp.float32) + head_bias(dist_w)
        p_w = masked_softmax(s_w, mask_w)
        o_w = jnp.einsum('bgrqk,bkgd->bqgrd', p_w.astype(v_w.dtype), v_w)
        return (o_c, o_s, o_w)

    o_c, o_s, o_w = lax.map(block, jnp.arange(T // QB) * QB)

    def unblock(o):
        return o.transpose(1, 0, 2, 3, 4, 5).reshape(B, T, NSA_HEADS, dh).astype(q.dtype)

    return unblock(o_c), unblock(o_s), unblock(o_w)


def memory_cross_attention(q_x, mem, g_mem, w_mem_kv):
    B, T, _ = q_x.shape
    M = mem.shape[1]
    kv = rms_norm(mem, g_mem) @ w_mem_kv
    k, v = jnp.split(kv, 2, axis=-1)
    q = q_x.reshape(B, T, X_HEADS, X_HEAD_DIM) * (X_HEAD_DIM ** -0.5)
    k = k.reshape(B, M, X_HEADS, X_HEAD_DIM)
    v = v.reshape(B, M, X_HEADS, X_HEAD_DIM)
    s = jnp.einsum('bthd,bmhd->bhtm', q, k).astype(jnp.float32)
    p = jax.nn.softmax(s, axis=-1).astype(v.dtype)
    return jnp.einsum('bhtm,bmhd->bthd', p, v).reshape(B, T, X_WIDTH)


def peer_ffn(h, w_q, sub_keys, u, v):
    B, T, D = h.shape
    tokens = h.reshape(-1, D)
    blocks = tokens.reshape(-1, PEER_TOKEN_BLOCK, D)
    n = PEER_TOKEN_BLOCK

    def block(xb):
        q = (xb @ w_q).reshape(n, PEER_HEADS, 2, PEER_HALF)
        s = jnp.einsum('nhpc,hpkc->nhpk', q, sub_keys).astype(jnp.float32)
        s_top, i_top = lax.top_k(s, PEER_TOPK)
        cand_s = (s_top[:, :, 0, :, None] + s_top[:, :, 1, None, :]).reshape(n, PEER_HEADS, -1)
        cand_i = (i_top[:, :, 0, :, None] * N_KEYS + i_top[:, :, 1, None, :]).reshape(n, PEER_HEADS, -1)
        best_s, best_pos = lax.top_k(cand_s, PEER_TOPK)
        ids = jnp.take_along_axis(cand_i, best_pos, axis=-1)
        gate = jax.nn.softmax(best_s, axis=-1)
        u_e = u[ids]
        act = jax.nn.gelu(jnp.einsum('nd,nhkd->nhk', xb, u_e).astype(jnp.float32), approximate=False)
        wgt = (gate * act).astype(xb.dtype)
        return jnp.einsum('nhk,nhkd->nd', wgt, v[ids])

    return lax.map(block, blocks).reshape(B, T, D)


def setup_inputs(seed: int = 0) -> dict:
    key = jax.random.key(seed)
    ks = jax.random.split(key, 24)

    def nrm(k, shape, scale):
        return jax.random.normal(k, shape, jnp.float32) * scale

    D = D_MODEL
    return {
        "x": nrm(ks[0], (BATCH, SEQ, D), 1.0),
        "mem": nrm(ks[1], (BATCH, MEM_LEN, D), 1.0),
        "rel_bias": nrm(ks[2], (N_BUCKETS, NSA_HEADS), 0.5),
        "g_mix": 1.0 + nrm(ks[3], (DEPTH, D), 0.05),
        "w_in": nrm(ks[4], (DEPTH, D, D_IN), D ** -0.5),
        "w_pool_grp": nrm(ks[5], (DEPTH, POOL_GROUPS, POOL_GROUP_DIM, POOL_GROUP_DIM), POOL_GROUP_DIM ** -0.5),
        "pool_scale": 1.0 + nrm(ks[6], (DEPTH, POOL_WIDTH), 0.1),
        "w_pool_out": nrm(ks[7], (DEPTH, POOL_WIDTH, D), POOL_WIDTH ** -0.5),
        "w_cmp_k": nrm(ks[8], (DEPTH, CMP_LEN * HEAD_DIM, HEAD_DIM), (CMP_LEN * HEAD_DIM) ** -0.5),
        "w_cmp_v": nrm(ks[9], (DEPTH, CMP_LEN * HEAD_DIM, HEAD_DIM), (CMP_LEN * HEAD_DIM) ** -0.5),
        "pe_k": nrm(ks[10], (DEPTH, CMP_LEN, HEAD_DIM), 0.5),
        "pe_v": nrm(ks[11], (DEPTH, CMP_LEN, HEAD_DIM), 0.5),
        "w_nsa_out": nrm(ks[12], (DEPTH, NSA_WIDTH, D), NSA_WIDTH ** -0.5),
        "g_mem": 1.0 + nrm(ks[13], (DEPTH, D), 0.05),
        "w_mem_kv": nrm(ks[14], (DEPTH, D, 2 * X_WIDTH), D ** -0.5),
        "w_x_out": nrm(ks[15], (DEPTH, X_WIDTH, D), X_WIDTH ** -0.5),
        "w_o": nrm(ks[16], (DEPTH, D, D), D ** -0.5),
        "g_ffn": 1.0 + nrm(ks[17], (DEPTH, D), 0.05),
        "w_peer_q": nrm(ks[18], (DEPTH, D, PEER_HEADS * PEER_QDIM), D ** -0.5),
        "peer_sub_keys": nrm(ks[19], (DEPTH, PEER_HEADS, 2, N_KEYS, PEER_HALF), PEER_HALF ** -0.5),
        "peer_u": nrm(ks[20], (DEPTH, N_EXPERTS, D), D ** -0.5),
        "peer_v": nrm(ks[21], (DEPTH, N_EXPERTS, D), 0.1),
        "g_final": 1.0 + nrm(ks[22], (D,), 0.05),
    }


def reference(x, mem, rel_bias, g_mix, w_in, w_pool_grp, pool_scale, w_pool_out,
              w_cmp_k, w_cmp_v, pe_k, pe_v, w_nsa_out, g_mem, w_mem_kv, w_x_out, w_o,
              g_ffn, w_peer_q, peer_sub_keys, peer_u, peer_v, g_final):
    B, T, D = x.shape
    for l in range(DEPTH):
        hn = rms_norm(x, g_mix[l])
        z = hn @ w_in[l]
        u_pool, q_nsa, kv_nsa, g_nsa, q_x, g_merge = jnp.split(z, IN_SPLITS, axis=-1)

        y_pool = pool_mixer(u_pool, w_pool_grp[l], pool_scale[l]) @ w_pool_out[l]

        kv = kv_nsa.reshape(B, T, 6, NSA_KV_HEADS, HEAD_DIM)
        o_c, o_s, o_w = nsa_attention(q_nsa.reshape(B, T, NSA_HEADS, HEAD_DIM),
                                      kv[:, :, 0], kv[:, :, 1], kv[:, :, 2], kv[:, :, 3],
                                      kv[:, :, 4], kv[:, :, 5],
                                      w_cmp_k[l], w_cmp_v[l], pe_k[l], pe_v[l], rel_bias)
        bg = jax.nn.sigmoid(g_nsa.reshape(B, T, 3, NSA_HEADS))[..., None]
        o_nsa = (bg[:, :, 0] * o_c + bg[:, :, 1] * o_s + bg[:, :, 2] * o_w).reshape(B, T, NSA_WIDTH)
        y_nsa = o_nsa @ w_nsa_out[l]

        y_mem = memory_cross_attention(q_x, mem, g_mem[l], w_mem_kv[l]) @ w_x_out[l]

        gm = jax.nn.sigmoid(g_merge.reshape(B, T, N_BRANCHES, D))
        merged = gm[:, :, 0] * y_pool + gm[:, :, 1] * y_nsa + gm[:, :, 2] * y_mem
        x = x + merged @ w_o[l]

        x = x + peer_ffn(rms_norm(x, g_ffn[l]), w_peer_q[l], peer_sub_keys[l], peer_u[l], peer_v[l])
    return rms_norm(x, g_final)
```

```python
import functools
import math

import jax
import jax.numpy as jnp
import numpy as np
from jax import lax
from jax.experimental import pallas as pl
from jax.experimental.pallas import tpu as pltpu

F32 = jnp.float32
BF16 = jnp.bfloat16

D_MODEL = 1024
RMS_EPS = 1e-6
NEG_INF = -1e30
FORCE = 1e6

POOL_WINDOWS = (2, 4, 8, 16)
POOL_GROUP_DIM = 128
POOL_WIDTH = 512
MAX_POOL_WINDOW = 16

NSA_HEADS = 16
NSA_KV_HEADS = 4
NSA_GROUP = 4
HEAD_DIM = 64
NSA_WIDTH = 1024
KV_WIDTH = 256
CMP_LEN = 32
CMP_STRIDE = 16
SEL_BLOCK = 64
N_SELECT = 16
WINDOW = 512

X_HEADS = 4
X_HEAD_DIM = 128
X_WIDTH = 512

N_BUCKETS = 32
MAX_DISTANCE = 128

PEER_HEADS = 8
PEER_HALF = 128
PEER_TOPK = 16

GM_OFF, GM_W = 0, 3 * D_MODEL
Q_OFF = GM_OFF + GM_W
KV_OFF = Q_OFF + NSA_WIDTH
POOL_OFF = KV_OFF + 6 * KV_WIDTH
QX_OFF = POOL_OFF + POOL_WIDTH
GN_OFF = QX_OFF + X_WIDTH
Z_WIDTH = GN_OFF + NSA_KV_HEADS * 128

LANES = 128
TQ = 128
CMP_NEAR = 32
CMP_PAD = 16
VMEM_LIMIT = 56 * 1024 * 1024


def _dot(a, b):
    return jnp.dot(a, b, preferred_element_type=F32)


def _dot_nt(a, b):
    return lax.dot_general(a, b, (((1,), (1,)), ((), ())), preferred_element_type=F32)


def _rms(xf, g):
    ms = jnp.mean(xf * xf, axis=-1, keepdims=True)
    return xf * lax.rsqrt(ms + RMS_EPS) * g


def _cparams(sem):
    return pltpu.CompilerParams(dimension_semantics=sem, vmem_limit_bytes=VMEM_LIMIT)


def _inproj_kernel(x_ref, g_ref, w_ref, z_ref, hn_ref):
    @pl.when(pl.program_id(1) == 0)
    def _():
        hn_ref[...] = _rms(x_ref[...], g_ref[...]).astype(BF16)

    z_ref[...] = _dot(hn_ref[...], w_ref[...]).astype(z_ref.dtype)


def _inproj(x2d, g_mix, w_in_p, tm=512, tn=1024):
    n = x2d.shape[0]
    return pl.pallas_call(
        _inproj_kernel,
        grid=(n // tm, Z_WIDTH // tn),
        in_specs=[
            pl.BlockSpec((tm, D_MODEL), lambda i, j: (i, 0)),
            pl.BlockSpec((1, D_MODEL), lambda i, j: (0, 0)),
            pl.BlockSpec((D_MODEL, tn), lambda i, j: (0, j)),
        ],
        out_specs=pl.BlockSpec((tm, tn), lambda i, j: (i, j)),
        out_shape=jax.ShapeDtypeStruct((n, Z_WIDTH), BF16),
        scratch_shapes=[pltpu.VMEM((tm, D_MODEL), BF16)],
        compiler_params=_cparams(("arbitrary", "arbitrary")),
        name="inproj",
    )(x2d, g_mix, w_in_p)


def _relayout_w_in(w_in):
    o = 0
    w_pool = w_in[:, o:o + POOL_WIDTH]; o += POOL_WIDTH
    w_q = w_in[:, o:o + NSA_WIDTH]; o += NSA_WIDTH
    w_kv = w_in[:, o:o + 6 * KV_WIDTH]; o += 6 * KV_WIDTH
    w_gn = w_in[:, o:o + 3 * NSA_HEADS]; o += 3 * NSA_HEADS
    w_qx = w_in[:, o:o + X_WIDTH]; o += X_WIDTH
    w_gm = w_in[:, o:o + 3 * D_MODEL]
    d = w_in.shape[0]
    gn = w_gn.reshape(d, 3, NSA_KV_HEADS, NSA_GROUP).transpose(0, 2, 1, 3).reshape(d, NSA_KV_HEADS, 3 * NSA_GROUP)
    gn = jnp.pad(gn, ((0, 0), (0, 0), (0, LANES - 3 * NSA_GROUP))).reshape(d, NSA_KV_HEADS * LANES)
    return jnp.concatenate([w_gm, w_q, w_kv, w_pool, w_qx, gn], axis=1).astype(BF16)


def _compress_kernel(ck_ref, cv_ref, pek_ref, pev_ref, wk_ref, wv_ref, kc_ref, vc_ref):
    half = (CMP_LEN // 2) * HEAD_DIM
    for c_ref, pe_ref, w_ref, o_ref in ((ck_ref, pek_ref, wk_ref, kc_ref), (cv_ref, pev_ref, wv_ref, vc_ref)):
        ch = c_ref[0, 0].astype(F32)
        nch = ch.shape[0]
        first = _dot((ch + pe_ref[0:1, :]).astype(BF16), w_ref[0:half, :])
        second = _dot((ch + pe_ref[1:2, :]).astype(BF16), w_ref[half:2 * half, :])
        o_ref[0, 0, 0:CMP_PAD, :] = jnp.zeros((CMP_PAD, HEAD_DIM), F32)
        o_ref[0, 0, CMP_PAD:CMP_PAD + nch, :] = first + pltpu.roll(second, nch - 1, 0)
        o_ref[0, 0, CMP_PAD + nch:2 * CMP_PAD + nch, :] = jnp.zeros((CMP_PAD, HEAD_DIM), F32)


def _compress(ck, cv, pe_k, pe_v, w_k, w_v):
    b, g, nch, cw = ck.shape
    out = jax.ShapeDtypeStruct((b, g, 2 * CMP_PAD + nch, HEAD_DIM), F32)
    cspec = pl.BlockSpec((1, 1, nch, cw), lambda i, j: (i, j, 0, 0))
    pspec = pl.BlockSpec((2, cw), lambda i, j: (0, 0))
    wspec = pl.BlockSpec((2 * cw, HEAD_DIM), lambda i, j: (0, 0))
    ospec = pl.BlockSpec((1, 1, 2 * CMP_PAD + nch, HEAD_DIM), lambda i, j: (i, j, 0, 0))
    return pl.pallas_call(
        _compress_kernel,
        grid=(b, g),
        in_specs=[cspec, cspec, pspec, pspec, wspec, wspec],
        out_specs=[ospec, ospec],
        out_shape=[out, out],
        compiler_params=_cparams(("arbitrary", "arbitrary")),
        name="compress",
    )(ck, cv, pe_k, pe_v, w_k, w_v)


def _t5_bucket_np(dist):
    n = np.maximum(dist, 0)
    max_exact = N_BUCKETS // 2
    nf = np.maximum(n, 1).astype(np.float32)
    large = max_exact + (np.log(nf / np.float32(max_exact)) / np.float32(math.log(MAX_DISTANCE / max_exact))
                         * np.float32(N_BUCKETS - max_exact)).astype(np.int32)
    large = np.minimum(large, N_BUCKETS - 1)
    return np.where(n < max_exact, n, large).astype(np.int32)


BT_DIAG, BT_PREV, BT_FAR, BT_WEDGE, BT_CMP = range(5)
N_BT = 5


def _bias_tile_constants():
    i = np.arange(TQ)[:, None]
    j = np.arange(LANES)[None, :]
    far = np.full((TQ, LANES), MAX_DISTANCE, np.int64)
    dists = [
        i - j,
        TQ + i - j,
        far,
        far,
        i - CMP_STRIDE * (j - CMP_PAD) - (CMP_LEN - 1),
    ]
    valid = [
        dists[0] >= 0,
        np.ones((TQ, LANES), bool),
        np.ones((TQ, LANES), bool),
        (i - j) < 0,
        (dists[4] >= 0) & (j < CMP_NEAR),
    ]
    bkt = np.stack([_t5_bucket_np(d) for d in dists]).astype(np.int32)
    madd = np.stack([np.where(v, 0.0, NEG_INF) for v in valid]).astype(np.float32)
    return bkt, madd


def _biasprep_kernel(rb_ref, bkt_ref, madd_ref, o_ref):
    h = pl.program_id(0)
    for t in range(N_BT):
        bkt = bkt_ref[t]
        acc = jnp.zeros(bkt.shape, F32)
        for bk in range(N_BUCKETS):
            acc = jnp.where(bkt == bk, rb_ref[bk, h], acc)
        o_ref[0, t, 0] = acc + madd_ref[t]


def _biasprep(rel_bias):
    bkt, madd = _bias_tile_constants()
    out = pl.pallas_call(
        _biasprep_kernel,
        grid=(NSA_HEADS,),
        in_specs=[
            pl.BlockSpec(memory_space=pltpu.SMEM),
            pl.BlockSpec((N_BT, TQ, LANES), lambda h: (0, 0, 0)),
            pl.BlockSpec((N_BT, TQ, LANES), lambda h: (0, 0, 0)),
        ],
        out_specs=pl.BlockSpec((1, N_BT, 1, TQ, LANES), lambda h: (h // NSA_GROUP, 0, h % NSA_GROUP, 0, 0)),
        out_shape=jax.ShapeDtypeStruct((NSA_KV_HEADS, N_BT, NSA_GROUP, TQ, LANES), F32),
        compiler_params=_cparams(("arbitrary",)),
        name="biasprep",
    )(rel_bias, jnp.asarray(bkt), jnp.asarray(madd))
    return out.reshape(NSA_KV_HEADS, N_BT, NSA_GROUP * TQ, LANES)


def _nsa_constants(t):
    n_cmp = (t - CMP_LEN) // CMP_STRIDE + 1
    n_sel = t // SEL_BLOCK
    nch = t // CMP_STRIDE
    c = np.arange(nch)[:, None]
    s = np.arange(LANES)[None, :]
    cs = c * CMP_STRIDE
    ss = s * SEL_BLOCK
    ov = (cs < ss + SEL_BLOCK) & (cs + CMP_LEN > ss) & (c < n_cmp) & (s < n_sel)
    zpad = np.zeros((CMP_PAD, LANES), bool)
    ov = np.concatenate([zpad, ov, zpad], axis=0).astype(np.float32)
    key = np.arange(t)[None, :]
    expand = ((key // SEL_BLOCK) == np.arange(LANES)[:, None]).astype(np.float32)
    return ov, expand


def _nsa_kernel(zq_ref, gn_ref, kc_ref, vc_ref, ks_ref, vs_ref, kw_ref, vw_ref, bt_ref, ov_ref, ex_ref,
                o_ref, m_sc, l_sc, acc_sc, selx_sc, *, seq):
    qi = pl.program_id(2)
    t0 = qi * TQ
    n_sel = seq // SEL_BLOCK
    n_top = min(N_SELECT, n_sel)
    nch = seq // CMP_STRIDE
    rows = NSA_GROUP * TQ

    q = zq_ref[...].astype(F32) * (HEAD_DIM ** -0.5)
    q4 = jnp.concatenate([q[:, r * HEAD_DIM:(r + 1) * HEAD_DIM] for r in range(NSA_GROUP)], axis=0).astype(BF16)

    row_t = t0 + lax.broadcasted_iota(jnp.int32, (TQ, 1), 0)

    c0 = pl.multiple_of(qi * (TQ // CMP_STRIDE), 8)
    kcf = kc_ref[0, 0, CMP_PAD:CMP_PAD + nch, :].astype(BF16)
    vcf = vc_ref[0, 0, CMP_PAD:CMP_PAD + nch, :].astype(BF16)
    kcn = kc_ref[0, 0, pl.ds(c0, CMP_NEAR), :].astype(BF16)
    vcn = vc_ref[0, 0, pl.ds(c0, CMP_NEAR), :].astype(BF16)
    far_bias = bt_ref[0, BT_FAR]
    s_far = _dot_nt(q4, kcf) + far_bias[:, 0:1]
    col = lax.broadcasted_iota(jnp.int32, (1, nch), 1)
    s_far = jnp.where(col < c0 - CMP_PAD, s_far, NEG_INF)
    s_near = _dot_nt(q4, kcn) + bt_ref[0, BT_CMP][:, 0:CMP_NEAR]
    coln = lax.broadcasted_iota(jnp.int32, (1, CMP_NEAR), 1)
    s_near = jnp.where(coln >= CMP_PAD - c0, s_near, NEG_INF)
    m_c = jnp.maximum(jnp.max(s_far, axis=1, keepdims=True), jnp.max(s_near, axis=1, keepdims=True))
    p_far = jnp.exp(s_far - m_c)
    p_near = jnp.exp(s_near - m_c)
    l_c = jnp.sum(p_far, axis=1, keepdims=True) + jnp.sum(p_near, axis=1, keepdims=True)
    has = jnp.concatenate([(row_t >= CMP_LEN - 1).astype(F32)] * NSA_GROUP, axis=0)
    inv_c = has / l_c
    o_cmp = (_dot(p_far.astype(BF16), vcf) + _dot(p_near.astype(BF16), vcn)) * inv_c

    pn_far = p_far * inv_c
    pn_near = p_near * inv_c
    ps_far = pn_far[0:TQ]
    ps_near = pn_near[0:TQ]
    for r in range(1, NSA_GROUP):
        ps_far = ps_far + pn_far[r * TQ:(r + 1) * TQ]
        ps_near = ps_near + pn_near[r * TQ:(r + 1) * TQ]

    def split_dot(p, ov):
        hi = p.astype(BF16)
        lo = (p - hi.astype(F32)).astype(BF16)
        return _dot(hi, ov) + _dot(lo, ov)

    ov_far = ov_ref[CMP_PAD:CMP_PAD + nch, :]
    ov_near = ov_ref[pl.ds(c0, CMP_NEAR), :]
    imp = split_dot(ps_far, ov_far) + split_dot(ps_near, ov_near)

    blk_q = lax.shift_right_logical(row_t, int(math.log2(SEL_BLOCK)))
    s_idx = lax.broadcasted_iota(jnp.int32, (1, LANES), 1)
    forced = (s_idx == 0) | (s_idx == blk_q) | (s_idx == blk_q - 1)
    imp = jnp.where(forced, FORCE, jnp.where(s_idx <= blk_q, imp, -FORCE))

    v = imp.T[0:n_sel, :]
    s_row = lax.broadcasted_iota(jnp.int32, (n_sel, 1), 0)
    rank = jnp.zeros((n_sel, TQ), F32)
    for sp in range(n_sel):
        other = v[sp:sp + 1, :]
        rank = rank + jnp.where(s_row > sp, jnp.where(other >= v, 1.0, 0.0), jnp.where(other > v, 1.0, 0.0))
    sel_t = jnp.where(rank < n_top, 1.0, 0.0)
    if n_sel < LANES:
        sel_t = jnp.concatenate([sel_t, jnp.zeros((LANES - n_sel, TQ), F32)], axis=0)
    sel = sel_t.T.astype(BF16)
    selx_sc[...] = (_dot(sel, ex_ref[...]) - 1.0) * (-NEG_INF)

    def init():
        m_sc[...] = jnp.full(m_sc.shape, NEG_INF, F32)
        l_sc[...] = jnp.zeros(l_sc.shape, F32)
        acc_sc[...] = jnp.zeros(acc_sc.shape, F32)

    def step(k_ref, v_ref, j, kind, use_sel):
        off = pl.multiple_of(j * TQ, TQ)
        kt = k_ref[0, 0, 0, pl.ds(off, TQ), :]
        vt = v_ref[0, 0, 0, pl.ds(off, TQ), :]
        s = _dot_nt(q4, kt) + bt_ref[0, kind]
        if use_sel:
            s = s + jnp.concatenate([selx_sc[:, pl.ds(off, TQ)]] * NSA_GROUP, axis=0)
        m_prev = m_sc[...]
        m_next = jnp.maximum(m_prev, jnp.max(s, axis=1, keepdims=True))
        alpha = jnp.exp(m_prev - m_next)
        p = jnp.exp(s - m_next)
        l_sc[...] = alpha * l_sc[...] + jnp.sum(p, axis=1, keepdims=True)
        acc_sc[...] = acc_sc[...] * alpha[:, 0:HEAD_DIM] + _dot(p.astype(BF16), vt)
        m_sc[...] = m_next

    def finish():
        return acc_sc[...] / l_sc[...][:, 0:HEAD_DIM]

    init()

    def far_body(j, carry):
        step(ks_ref, vs_ref, j, BT_FAR, True)
        return carry

    lax.fori_loop(0, jnp.maximum(qi - 1, 0), far_body, 0)

    @pl.when(qi >= 1)
    def _():
        step(ks_ref, vs_ref, qi - 1, BT_PREV, True)

    step(ks_ref, vs_ref, qi, BT_DIAG, True)
    o_sel = finish()

    init()
    n_win = WINDOW // TQ
    for d in range(n_win, -1, -1):
        kind = BT_WEDGE if d == n_win else (BT_DIAG if d == 0 else (BT_PREV if d == 1 else BT_FAR))
        if d == 0:
            step(kw_ref, vw_ref, qi, kind, False)
        else:
            @pl.when(qi >= d)
            def _(d=d, kind=kind):
                step(kw_ref, vw_ref, qi - d, kind, False)
    o_win = finish()

    gates = jax.nn.sigmoid(gn_ref[...].astype(F32))
    outs = []
    for r in range(NSA_GROUP):
        sl = slice(r * TQ, (r + 1) * TQ)
        o_r = (gates[:, r:r + 1] * o_cmp[sl]
               + gates[:, NSA_GROUP + r:NSA_GROUP + r + 1] * o_sel[sl]
               + gates[:, 2 * NSA_GROUP + r:2 * NSA_GROUP + r + 1] * o_win[sl])
        outs.append(o_r)
    o_ref[...] = jnp.concatenate(outs, axis=1).astype(o_ref.dtype)


def _nsa(z, kvc, kv6, btiles, batch, seq):
    nq = seq // TQ
    n = batch * seq
    ov, expand = _nsa_constants(seq)
    nchp = kvc[0].shape[2]
    rows = NSA_GROUP * TQ
    qblk = Q_OFF // KV_WIDTH
    gblk = GN_OFF // LANES

    def kvspec(k):
        return pl.BlockSpec((1, 1, 1, seq, HEAD_DIM), lambda b, g, i, k=k: (k, b, g, 0, 0))

    cspec = pl.BlockSpec((1, 1, nchp, HEAD_DIM), lambda b, g, i: (b, g, 0, 0))
    return pl.pallas_call(
        functools.partial(_nsa_kernel, seq=seq),
        grid=(batch, NSA_KV_HEADS, nq),
        in_specs=[
            pl.BlockSpec((TQ, KV_WIDTH), lambda b, g, i: (b * nq + i, qblk + g)),
            pl.BlockSpec((TQ, LANES), lambda b, g, i: (b * nq + i, gblk + g)),
            cspec, cspec,
            kvspec(2), kvspec(3), kvspec(4), kvspec(5),
            pl.BlockSpec((1, N_BT, rows, LANES), lambda b, g, i: (g, 0, 0, 0)),
            pl.BlockSpec(ov.shape, lambda b, g, i: (0, 0)),
            pl.BlockSpec(expand.shape, lambda b, g, i: (0, 0)),
        ],
        out_specs=pl.BlockSpec((TQ, KV_WIDTH), lambda b, g, i: (b * nq + i, g)),
        out_shape=jax.ShapeDtypeStruct((n, NSA_WIDTH), BF16),
        scratch_shapes=[
            pltpu.VMEM((rows, LANES), F32),
            pltpu.VMEM((rows, LANES), F32),
            pltpu.VMEM((rows, HEAD_DIM), F32),
            pltpu.VMEM((TQ, seq), F32),
        ],
        compiler_params=_cparams(("arbitrary", "arbitrary", "arbitrary")),
        name="nsa",
    )(z, z, kvc[0], kvc[1], kv6, kv6, kv6, kv6, btiles, jnp.asarray(ov, BF16), jnp.asarray(expand, BF16))


def _memkv_kernel(mem_ref, g_ref, w_ref, k_ref, v_ref):
    hn = _rms(mem_ref[0], g_ref[...]).astype(BF16)
    kv = _dot(hn, w_ref[...])
    k_ref[0] = kv[:, 0:X_WIDTH].astype(BF16)
    v_ref[0] = kv[:, X_WIDTH:2 * X_WIDTH].astype(BF16)


def _memkv(mem, g_mem, w_mem_kv):
    b, m, d = mem.shape
    out = jax.ShapeDtypeStruct((b, m, X_WIDTH), BF16)
    ospec = pl.BlockSpec((1, m, X_WIDTH), lambda i: (i, 0, 0))
    return pl.pallas_call(
        _memkv_kernel,
        grid=(b,),
        in_specs=[
            pl.BlockSpec((1, m, d), lambda i: (i, 0, 0)),
            pl.BlockSpec((1, d), lambda i: (0, 0)),
            pl.BlockSpec((d, 2 * X_WIDTH), lambda i: (0, 0)),
        ],
        out_specs=[ospec, ospec],
        out_shape=[out, out],
        compiler_params=_cparams(("arbitrary",)),
        name="memkv",
    )(mem, g_mem, w_mem_kv)


def _merge_kernel(gm_ref, pool_ref, prev_ref, qx_ref, onsa_ref, x_ref, km_ref, vm_ref,
                  wgrp_ref, pscale_ref, wpo_ref, wno_ref, wxo_ref, wo_ref, gffn_ref, wpq_ref, sk_ref,
                  x1_ref, xb_ref, st_ref, *, seq):
    i = pl.program_id(0)
    tm = x_ref.shape[0]
    tpos = lax.rem(i * tm, seq) + lax.broadcasted_iota(jnp.int32, (tm, 1), 0)

    u_cur = pool_ref[...].astype(F32)
    u_prev = jnp.where(lax.rem(i * tm, seq) == 0, 0.0, prev_ref[...].astype(F32))
    ext = jnp.concatenate([u_prev, u_cur], axis=0)
    mixed = []
    for gi, w in enumerate(POOL_WINDOWS):
        cs = slice(gi * POOL_GROUP_DIM, (gi + 1) * POOL_GROUP_DIM)
        e = ext[:, cs]
        tot = e[MAX_POOL_WINDOW:MAX_POOL_WINDOW + tm]
        for k in range(1, w):
            tot = tot + e[MAX_POOL_WINDOW - k:MAX_POOL_WINDOW - k + tm]
        cnt = jnp.minimum(tpos + 1, w).astype(F32)
        pooled = tot / cnt - u_cur[:, cs]
        mixed.append(_dot(pooled.astype(BF16), wgrp_ref[gi]))
    mixed = jnp.concatenate(mixed, axis=1) * pscale_ref[...]
    y_pool = _dot(mixed.astype(BF16), wpo_ref[...])

    qx = qx_ref[...]
    km = km_ref[0]
    vm = vm_ref[0]
    o_mem = []
    for h in range(X_HEADS):
        hs = slice(h * X_HEAD_DIM, (h + 1) * X_HEAD_DIM)
        s = _dot_nt(qx[:, hs], km[:, hs]) * (X_HEAD_DIM ** -0.5)
        p = jnp.exp(s - jnp.max(s, axis=1, keepdims=True))
        o_mem.append(_dot(p.astype(BF16), vm[:, hs]) / jnp.sum(p, axis=1, keepdims=True))
    y_mem = _dot(jnp.concatenate(o_mem, axis=1).astype(BF16), wxo_ref[...])

    y_nsa = _dot(onsa_ref[...], wno_ref[...])

    gm = jax.nn.sigmoid(gm_ref[...].astype(F32))
    merged = gm[:, 0:D_MODEL] * y_pool + gm[:, D_MODEL:2 * D_MODEL] * y_nsa + gm[:, 2 * D_MODEL:] * y_mem
    x1 = x_ref[...] + _dot(merged.astype(BF16), wo_ref[...])
    x1_ref[...] = x1

    xb = _rms(x1, gffn_ref[...]).astype(BF16)
    xb_ref[...] = xb
    qp = _dot(xb, wpq_ref[...]).astype(BF16)
    for hp in range(2 * PEER_HEADS):
        st_ref[hp] = _dot_nt(sk_ref[hp], qp[:, hp * PEER_HALF:(hp + 1) * PEER_HALF])


def _merge(z, o_nsa, x2d, km, vm, w, seq, tm=256):
    n = x2d.shape[0]
    nkeys = w["sk"].shape[1]
    pblk = POOL_OFF // POOL_WIDTH
    prev_per_tile = tm // MAX_POOL_WINDOW

    def const(a):
        nd = a.ndim
        return pl.BlockSpec(a.shape, lambda i, nd=nd: (0,) * nd)

    weights = [w["wgrp"], w["pscale"], w["wpo"], w["wno"], w["wxo"], w["wo"], w["gffn"], w["wpq"], w["sk"]]
    return pl.pallas_call(
        functools.partial(_merge_kernel, seq=seq),
        grid=(n // tm,),
        in_specs=[
            pl.BlockSpec((tm, GM_W), lambda i: (i, 0)),
            pl.BlockSpec((tm, POOL_WIDTH), lambda i: (i, pblk)),
            pl.BlockSpec((MAX_POOL_WINDOW, POOL_WIDTH),
                         lambda i: (jnp.maximum(i * prev_per_tile - 1, 0), pblk)),
            pl.BlockSpec((tm, X_WIDTH), lambda i: (i, QX_OFF // X_WIDTH)),
            pl.BlockSpec((tm, NSA_WIDTH), lambda i: (i, 0)),
            pl.BlockSpec((tm, D_MODEL), lambda i: (i, 0)),
            pl.BlockSpec((1,) + km.shape[1:], lambda i: ((i * tm) // seq, 0, 0)),
            pl.BlockSpec((1,) + vm.shape[1:], lambda i: ((i * tm) // seq, 0, 0)),
        ] + [const(a) for a in weights],
        out_specs=[
            pl.BlockSpec((tm, D_MODEL), lambda i: (i, 0)),
            pl.BlockSpec((tm, D_MODEL), lambda i: (i, 0)),
            pl.BlockSpec((2 * PEER_HEADS, nkeys, tm), lambda i: (0, 0, i)),
        ],
        out_shape=[
            jax.ShapeDtypeStruct((n, D_MODEL), F32),
            jax.ShapeDtypeStruct((n, D_MODEL), BF16),
            jax.ShapeDtypeStruct((2 * PEER_HEADS, nkeys, n), F32),
        ],
        compiler_params=_cparams(("arbitrary",)),
        name="merge",
    )(z, z, z, z, o_nsa, x2d, km, vm, *weights)


def _top_desc(s, count):
    rows = s.shape[0]
    ridx = lax.broadcasted_iota(jnp.int32, s.shape, 0)
    out = []
    for _ in range(count):
        m = jnp.max(s, axis=0, keepdims=True)
        out.append(m)
        first = jnp.min(jnp.where(s == m, ridx, rows), axis=0, keepdims=True)
        s = jnp.where(ridx == first, -jnp.inf, s)
    return out


def _route_kernel(st_ref, rt_ref):
    tn = st_ref.shape[2]
    taus, m1s, m2s, zinvs = [], [], [], []
    for h in range(PEER_HEADS):
        a = _top_desc(st_ref[2 * h], PEER_TOPK)
        b = _top_desc(st_ref[2 * h + 1], PEER_TOPK)
        cand = [a[i] + b[j] for i in range(PEER_TOPK) for j in range(PEER_TOPK) if (i + 1) * (j + 1) <= PEER_TOPK]
        pad = (-len(cand)) % 8
        cand = jnp.concatenate(cand + [jnp.full((pad, tn), -jnp.inf, F32)], axis=0)
        best = _top_desc(cand, PEER_TOPK)
        z = jnp.zeros((1, tn), F32)
        for bs in best:
            z = z + jnp.exp(bs - best[0])
        taus.append(best[-1]); m1s.append(a[0]); m2s.append(b[0]); zinvs.append(1.0 / z)
    for k, vals in enumerate((taus, m1s, m2s, zinvs)):
        rt_ref[k] = jnp.concatenate(vals, axis=0)


def _route(st, tn=256):
    hp, nkeys, n = st.shape
    return pl.pallas_call(
        _route_kernel,
        grid=(n // tn,),
        in_specs=[pl.BlockSpec((hp, nkeys, tn), lambda i: (0, 0, i))],
        out_specs=pl.BlockSpec((4, PEER_HEADS, tn), lambda i: (0, 0, i)),
        out_shape=jax.ShapeDtypeStruct((4, PEER_HEADS, n), F32),
        compiler_params=_cparams(("arbitrary",)),
        name="route",
    )(st)


def _peer_kernel(xb_ref, st_ref, rt_ref, u_ref, vt_ref, x1_ref, gfin_ref, y_ref, acc_sc, e1_sc, e2_sc, *, k1_per_chunk):
    c = pl.program_id(1)
    nkeys = st_ref.shape[1]

    @pl.when(c == 0)
    def _():
        acc_sc[...] = jnp.zeros(acc_sc.shape, F32)
        for h in range(PEER_HEADS):
            e1_sc[h] = jnp.exp(st_ref[2 * h] - rt_ref[1, h:h + 1, :]) * rt_ref[3, h:h + 1, :]
            e2_sc[h] = jnp.exp(st_ref[2 * h + 1] - rt_ref[2, h:h + 1, :])

    act = _dot_nt(u_ref[...], xb_ref[...])
    parts = []
    for k in range(k1_per_chunk):
        k1 = c * k1_per_chunk + k
        wsum = None
        for h in range(PEER_HEADS):
            s1 = st_ref[2 * h, pl.ds(k1, 1), :]
            e1 = e1_sc[h, pl.ds(k1, 1), :]
            term = jnp.where(s1 + st_ref[2 * h + 1] >= rt_ref[0, h:h + 1, :], e2_sc[h], 0.0) * e1
            wsum = term if wsum is None else wsum + term
        a = act[k * nkeys:(k + 1) * nkeys]
        gelu = 0.5 * a * (1.0 + lax.erf(a * (2.0 ** -0.5)))
        parts.append((gelu * wsum).astype(BF16))
    acc_sc[...] += _dot(vt_ref[...], jnp.concatenate(parts, axis=0))

    @pl.when(c == pl.num_programs(1) - 1)
    def _():
        x2 = x1_ref[...] + acc_sc[...].T
        y_ref[...] = _rms(x2, gfin_ref[...])


def _peer(xb, st, rt, u_bf, vt_bf, x1, g_final, tn=256, k1_per_chunk=8):
    n = xb.shape[0]
    hp, nkeys, _ = st.shape
    ne = u_bf.shape[0]
    ce = k1_per_chunk * nkeys
    return pl.pallas_call(
        functools.partial(_peer_kernel, k1_per_chunk=k1_per_chunk),
        grid=(n // tn, ne // ce),
        in_specs=[
            pl.BlockSpec((tn, D_MODEL), lambda i, c: (i, 0)),
            pl.BlockSpec((hp, nkeys, tn), lambda i, c: (0, 0, i)),
            pl.BlockSpec((4, PEER_HEADS, tn), lambda i, c: (0, 0, i)),
            pl.BlockSpec((ce, D_MODEL), lambda i, c: (c, 0)),
            pl.BlockSpec((D_MODEL, ce), lambda i, c: (0, c)),
            pl.BlockSpec((tn, D_MODEL), lambda i, c: (i, 0)),
            pl.BlockSpec((1, D_MODEL), lambda i, c: (0, 0)),
        ],
        out_specs=pl.BlockSpec((tn, D_MODEL), lambda i, c: (i, 0)),
        out_shape=jax.ShapeDtypeStruct((n, D_MODEL), F32),
        scratch_shapes=[
            pltpu.VMEM((D_MODEL, tn), F32),
            pltpu.VMEM((PEER_HEADS, nkeys, tn), F32),
            pltpu.VMEM((PEER_HEADS, nkeys, tn), F32),
        ],
        compiler_params=_cparams(("arbitrary", "arbitrary")),
        name="peer",
    )(xb, st, rt, u_bf, vt_bf, x1, g_final)


def _layer(x, mem, rel_bias, g_mix, w_in, w_pool_grp, pool_scale, w_pool_out, w_cmp_k, w_cmp_v, pe_k, pe_v,
           w_nsa_out, g_mem, w_mem_kv, w_x_out, w_o, g_ffn, w_peer_q, peer_sub_keys, peer_u, peer_v):
    batch, seq, d = x.shape
    n = batch * seq
    x2d = x.reshape(n, d)

    z = _inproj(x2d, g_mix.reshape(1, d), _relayout_w_in(w_in))

    kv6 = z[:, KV_OFF:KV_OFF + 6 * KV_WIDTH].reshape(batch, seq, 6, NSA_KV_HEADS, HEAD_DIM).transpose(2, 0, 3, 1, 4)
    nch = seq // CMP_STRIDE
    chunk_w = CMP_STRIDE * HEAD_DIM
    kvc = _compress(kv6[0].reshape(batch, NSA_KV_HEADS, nch, chunk_w), kv6[1].reshape(batch, NSA_KV_HEADS, nch, chunk_w),
                    pe_k.reshape(2, chunk_w), pe_v.reshape(2, chunk_w), w_cmp_k.astype(BF16), w_cmp_v.astype(BF16))
    o_nsa = _nsa(z, kvc, kv6, _biasprep(rel_bias), batch, seq)

    km, vm = _memkv(mem, g_mem.reshape(1, d), w_mem_kv.astype(BF16))

    nkeys = peer_sub_keys.shape[2]
    weights = dict(
        wgrp=w_pool_grp.astype(BF16), pscale=pool_scale.reshape(1, POOL_WIDTH), wpo=w_pool_out.astype(BF16),
        wno=w_nsa_out.astype(BF16), wxo=w_x_out.astype(BF16), wo=w_o.astype(BF16), gffn=g_ffn.reshape(1, d),
        wpq=w_peer_q.astype(BF16), sk=peer_sub_keys.reshape(2 * PEER_HEADS, nkeys, PEER_HALF).astype(BF16))
    x1, xb, st = _merge(z, o_nsa, x2d, km, vm, weights, seq)
    rt = _route(st)
    return x1, xb, st, rt, peer_u.astype(BF16), peer_v.T.astype(BF16)


def kernel(x, mem, rel_bias, g_mix, w_in, w_pool_grp, pool_scale, w_pool_out, w_cmp_k, w_cmp_v, pe_k, pe_v, w_nsa_out, g_mem, w_mem_kv, w_x_out, w_o, g_ffn, w_peer_q, peer_sub_keys, peer_u, peer_v, g_final):
    assert g_mix.shape[0] == 1, "single layer"
    x1, xb, st, rt, u_bf, vt_bf = _layer(
        x, mem, rel_bias, g_mix[0], w_in[0], w_pool_grp[0], pool_scale[0], w_pool_out[0], w_cmp_k[0], w_cmp_v[0],
        pe_k[0], pe_v[0], w_nsa_out[0], g_mem[0], w_mem_kv[0], w_x_out[0], w_o[0], g_ffn[0], w_peer_q[0],
        peer_sub_keys[0], peer_u[0], peer_v[0])
    y = _peer(xb, st, rt, u_bf, vt_bf, x1, g_final.reshape(1, -1))
    return y.reshape(x.shape)
```

```python
import functools
import math

import jax
import jax.numpy as jnp
import numpy as np
from jax import lax
from jax.experimental import pallas as pl
from jax.experimental.pallas import tpu as pltpu

F32 = jnp.float32
BF16 = jnp.bfloat16

D_MODEL = 1024
RMS_EPS = 1e-6
NEG_INF = -1e30
FORCE = 1e6

POOL_WINDOWS = (2, 4, 8, 16)
POOL_GROUP_DIM = 128
POOL_WIDTH = 512
MAX_POOL_WINDOW = 16

NSA_HEADS = 16
NSA_KV_HEADS = 4
NSA_GROUP = 4
HEAD_DIM = 64
NSA_WIDTH = 1024
KV_WIDTH = 256
CMP_LEN = 32
CMP_STRIDE = 16
SEL_BLOCK = 64
N_SELECT = 16
WINDOW = 512

X_HEADS = 4
X_HEAD_DIM = 128
X_WIDTH = 512

N_BUCKETS = 32
MAX_DISTANCE = 128

PEER_HEADS = 8
PEER_HALF = 128
PEER_TOPK = 16

GM_OFF, GM_W = 0, 3 * D_MODEL
Q_OFF = GM_OFF + GM_W
KV_OFF = Q_OFF + NSA_WIDTH
POOL_OFF = KV_OFF + 6 * KV_WIDTH
QX_OFF = POOL_OFF + POOL_WIDTH
GN_OFF = QX_OFF + X_WIDTH
Z_WIDTH = GN_OFF + NSA_KV_HEADS * 128

LANES = 128
TQ = 128
FAR_CHUNK = 512
CMP_NEAR = 32
CMP_PAD = 16
VMEM_LIMIT = 56 * 1024 * 1024


def _dot(a, b):
    return jnp.dot(a, b, preferred_element_type=F32)


def _dot_nt(a, b):
    return lax.dot_general(a, b, (((1,), (1,)), ((), ())), preferred_element_type=F32)


def _rms(xf, g):
    ms = jnp.mean(xf * xf, axis=-1, keepdims=True)
    return xf * lax.rsqrt(ms + RMS_EPS) * g


def _cparams(sem):
    return pltpu.CompilerParams(dimension_semantics=sem, vmem_limit_bytes=VMEM_LIMIT)


def _inproj_kernel(x_ref, g_ref, w_ref, z_ref, hn_ref):
    @pl.when(pl.program_id(1) == 0)
    def _():
        hn_ref[...] = _rms(x_ref[...], g_ref[...]).astype(BF16)

    z_ref[...] = _dot(hn_ref[...], w_ref[...]).astype(z_ref.dtype)


def _inproj(x2d, g_mix, w_in_p, tm=512, tn=1024):
    n = x2d.shape[0]
    return pl.pallas_call(
        _inproj_kernel,
        grid=(n // tm, Z_WIDTH // tn),
        in_specs=[
            pl.BlockSpec((tm, D_MODEL), lambda i, j: (i, 0)),
            pl.BlockSpec((1, D_MODEL), lambda i, j: (0, 0)),
            pl.BlockSpec((D_MODEL, tn), lambda i, j: (0, j)),
        ],
        out_specs=pl.BlockSpec((tm, tn), lambda i, j: (i, j)),
        out_shape=jax.ShapeDtypeStruct((n, Z_WIDTH), BF16),
        scratch_shapes=[pltpu.VMEM((tm, D_MODEL), BF16)],
        compiler_params=_cparams(("arbitrary", "arbitrary")),
        name="inproj",
    )(x2d, g_mix, w_in_p)


def _relayout_w_in(w_in):
    o = 0
    w_pool = w_in[:, o:o + POOL_WIDTH]; o += POOL_WIDTH
    w_q = w_in[:, o:o + NSA_WIDTH]; o += NSA_WIDTH
    w_kv = w_in[:, o:o + 6 * KV_WIDTH]; o += 6 * KV_WIDTH
    w_gn = w_in[:, o:o + 3 * NSA_HEADS]; o += 3 * NSA_HEADS
    w_qx = w_in[:, o:o + X_WIDTH]; o += X_WIDTH
    w_gm = w_in[:, o:o + 3 * D_MODEL]
    d = w_in.shape[0]
    gn = w_gn.reshape(d, 3, NSA_KV_HEADS, NSA_GROUP).transpose(0, 2, 1, 3).reshape(d, NSA_KV_HEADS, 3 * NSA_GROUP)
    gn = jnp.pad(gn, ((0, 0), (0, 0), (0, LANES - 3 * NSA_GROUP))).reshape(d, NSA_KV_HEADS * LANES)
    return jnp.concatenate([w_gm, w_q, w_kv, w_pool, w_qx, gn], axis=1).astype(BF16)


def _compress_kernel(ck_ref, cv_ref, pek_ref, pev_ref, wk_ref, wv_ref, kc_ref, vc_ref):
    half = (CMP_LEN // 2) * HEAD_DIM
    for c_ref, pe_ref, w_ref, o_ref in ((ck_ref, pek_ref, wk_ref, kc_ref), (cv_ref, pev_ref, wv_ref, vc_ref)):
        ch = c_ref[0, 0].astype(F32)
        nch = ch.shape[0]
        first = _dot((ch + pe_ref[0:1, :]).astype(BF16), w_ref[0:half, :])
        second = _dot((ch + pe_ref[1:2, :]).astype(BF16), w_ref[half:2 * half, :])
        o_ref[0, 0, 0:CMP_PAD, :] = jnp.zeros((CMP_PAD, HEAD_DIM), F32)
        o_ref[0, 0, CMP_PAD:CMP_PAD + nch, :] = first + pltpu.roll(second, nch - 1, 0)
        o_ref[0, 0, CMP_PAD + nch:2 * CMP_PAD + nch, :] = jnp.zeros((CMP_PAD, HEAD_DIM), F32)


def _compress(ck, cv, pe_k, pe_v, w_k, w_v):
    b, g, nch, cw = ck.shape
    out = jax.ShapeDtypeStruct((b, g, 2 * CMP_PAD + nch, HEAD_DIM), F32)
    cspec = pl.BlockSpec((1, 1, nch, cw), lambda i, j: (i, j, 0, 0))
    pspec = pl.BlockSpec((2, cw), lambda i, j: (0, 0))
    wspec = pl.BlockSpec((2 * cw, HEAD_DIM), lambda i, j: (0, 0))
    ospec = pl.BlockSpec((1, 1, 2 * CMP_PAD + nch, HEAD_DIM), lambda i, j: (i, j, 0, 0))
    return pl.pallas_call(
        _compress_kernel,
        grid=(b, g),
        in_specs=[cspec, cspec, pspec, pspec, wspec, wspec],
        out_specs=[ospec, ospec],
        out_shape=[out, out],
        compiler_params=_cparams(("arbitrary", "arbitrary")),
        name="compress",
    )(ck, cv, pe_k, pe_v, w_k, w_v)


def _t5_bucket_np(dist):
    n = np.maximum(dist, 0)
    max_exact = N_BUCKETS // 2
    nf = np.maximum(n, 1).astype(np.float32)
    large = max_exact + (np.log(nf / np.float32(max_exact)) / np.float32(math.log(MAX_DISTANCE / max_exact))
                         * np.float32(N_BUCKETS - max_exact)).astype(np.int32)
    large = np.minimum(large, N_BUCKETS - 1)
    return np.where(n < max_exact, n, large).astype(np.int32)


BT_DIAG, BT_PREV, BT_FAR, BT_WEDGE, BT_CMP = range(5)
N_BT = 5


def _bias_tile_constants():
    i = np.arange(TQ)[:, None]
    j = np.arange(LANES)[None, :]
    far = np.full((TQ, LANES), MAX_DISTANCE, np.int64)
    dists = [
        i - j,
        TQ + i - j,
        far,
        far,
        i - CMP_STRIDE * (j - CMP_PAD) - (CMP_LEN - 1),
    ]
    valid = [
        dists[0] >= 0,
        np.ones((TQ, LANES), bool),
        np.ones((TQ, LANES), bool),
        (i - j) < 0,
        (dists[4] >= 0) & (j < CMP_NEAR),
    ]
    bkt = np.stack([_t5_bucket_np(d) for d in dists]).astype(np.int32)
    madd = np.stack([np.where(v, 0.0, NEG_INF) for v in valid]).astype(np.float32)
    return bkt, madd


def _biasprep_kernel(rb_ref, bkt_ref, madd_ref, o_ref):
    h = pl.program_id(0)
    for t in range(N_BT):
        bkt = bkt_ref[t]
        acc = jnp.zeros(bkt.shape, F32)
        for bk in range(N_BUCKETS):
            acc = jnp.where(bkt == bk, rb_ref[bk, h], acc)
        o_ref[0, t, 0] = acc + madd_ref[t]


def _biasprep(rel_bias):
    bkt, madd = _bias_tile_constants()
    out = pl.pallas_call(
        _biasprep_kernel,
        grid=(NSA_HEADS,),
        in_specs=[
            pl.BlockSpec(memory_space=pltpu.SMEM),
            pl.BlockSpec((N_BT, TQ, LANES), lambda h: (0, 0, 0)),
            pl.BlockSpec((N_BT, TQ, LANES), lambda h: (0, 0, 0)),
        ],
        out_specs=pl.BlockSpec((1, N_BT, 1, TQ, LANES), lambda h: (h // NSA_GROUP, 0, h % NSA_GROUP, 0, 0)),
        out_shape=jax.ShapeDtypeStruct((NSA_KV_HEADS, N_BT, NSA_GROUP, TQ, LANES), F32),
        compiler_params=_cparams(("arbitrary",)),
        name="biasprep",
    )(rel_bias, jnp.asarray(bkt), jnp.asarray(madd))
    return out.reshape(NSA_KV_HEADS, N_BT, NSA_GROUP * TQ, LANES)


def _nsa_constants(t):
    n_cmp = (t - CMP_LEN) // CMP_STRIDE + 1
    n_sel = t // SEL_BLOCK
    nch = t // CMP_STRIDE
    c = np.arange(nch)[:, None]
    s = np.arange(LANES)[None, :]
    cs = c * CMP_STRIDE
    ss = s * SEL_BLOCK
    ov = (cs < ss + SEL_BLOCK) & (cs + CMP_LEN > ss) & (c < n_cmp) & (s < n_sel)
    zpad = np.zeros((CMP_PAD, LANES), bool)
    ov = np.concatenate([zpad, ov, zpad], axis=0).astype(np.float32)
    key = np.arange(t)[None, :]
    expand = ((key // SEL_BLOCK) == np.arange(LANES)[:, None]).astype(np.float32)
    return ov, expand


def _nsa_kernel(zq_ref, gn_ref, kc_ref, vc_ref, ks_ref, vs_ref, kw_ref, vw_ref, bt_ref, ov_ref, ex_ref,
                o_ref, m_sc, l_sc, acc_sc, selx_sc, *, seq):
    qi = pl.program_id(2)
    t0 = qi * TQ
    n_sel = seq // SEL_BLOCK
    n_top = min(N_SELECT, n_sel)
    nch = seq // CMP_STRIDE
    rows = NSA_GROUP * TQ

    q = zq_ref[...].astype(F32) * (HEAD_DIM ** -0.5)
    q4 = jnp.concatenate([q[:, r * HEAD_DIM:(r + 1) * HEAD_DIM] for r in range(NSA_GROUP)], axis=0).astype(BF16)

    row_t = t0 + lax.broadcasted_iota(jnp.int32, (TQ, 1), 0)

    c0 = pl.multiple_of(qi * (TQ // CMP_STRIDE), 8)
    kcf = kc_ref[0, 0, CMP_PAD:CMP_PAD + nch, :].astype(BF16)
    vcf = vc_ref[0, 0, CMP_PAD:CMP_PAD + nch, :].astype(BF16)
    kcn = kc_ref[0, 0, pl.ds(c0, CMP_NEAR), :].astype(BF16)
    vcn = vc_ref[0, 0, pl.ds(c0, CMP_NEAR), :].astype(BF16)
    far_bias = bt_ref[0, BT_FAR]
    s_far = _dot_nt(q4, kcf) + far_bias[:, 0:1]
    col = lax.broadcasted_iota(jnp.int32, (1, nch), 1)
    s_far = jnp.where(col < c0 - CMP_PAD, s_far, NEG_INF)
    s_near = _dot_nt(q4, kcn) + bt_ref[0, BT_CMP][:, 0:CMP_NEAR]
    coln = lax.broadcasted_iota(jnp.int32, (1, CMP_NEAR), 1)
    s_near = jnp.where(coln >= CMP_PAD - c0, s_near, NEG_INF)
    m_c = jnp.maximum(jnp.max(s_far, axis=1, keepdims=True), jnp.max(s_near, axis=1, keepdims=True))
    p_far = jnp.exp(s_far - m_c)
    p_near = jnp.exp(s_near - m_c)
    l_c = jnp.sum(p_far, axis=1, keepdims=True) + jnp.sum(p_near, axis=1, keepdims=True)
    has = jnp.concatenate([(row_t >= CMP_LEN - 1).astype(F32)] * NSA_GROUP, axis=0)
    inv_c = has / l_c
    o_cmp = (_dot(p_far.astype(BF16), vcf) + _dot(p_near.astype(BF16), vcn)) * inv_c

    pn_far = p_far * inv_c
    pn_near = p_near * inv_c
    ps_far = pn_far[0:TQ]
    ps_near = pn_near[0:TQ]
    for r in range(1, NSA_GROUP):
        ps_far = ps_far + pn_far[r * TQ:(r + 1) * TQ]
        ps_near = ps_near + pn_near[r * TQ:(r + 1) * TQ]

    def split_dot(p, ov):
        hi = p.astype(BF16)
        lo = (p - hi.astype(F32)).astype(BF16)
        return _dot(hi, ov) + _dot(lo, ov)

    ov_far = ov_ref[CMP_PAD:CMP_PAD + nch, :]
    ov_near = ov_ref[pl.ds(c0, CMP_NEAR), :]
    imp = split_dot(ps_far, ov_far) + split_dot(ps_near, ov_near)

    blk_q = lax.shift_right_logical(row_t, int(math.log2(SEL_BLOCK)))
    s_idx = lax.broadcasted_iota(jnp.int32, (1, LANES), 1)
    forced = (s_idx == 0) | (s_idx == blk_q) | (s_idx == blk_q - 1)
    imp = jnp.where(forced, FORCE, jnp.where(s_idx <= blk_q, imp, -FORCE))

    v = imp.T[0:n_sel, :]
    s_row = lax.broadcasted_iota(jnp.int32, (n_sel, 1), 0)
    rank = jnp.zeros((n_sel, TQ), F32)
    for sp in range(n_sel):
        other = v[sp:sp + 1, :]
        rank = rank + jnp.where(s_row > sp, jnp.where(other >= v, 1.0, 0.0), jnp.where(other > v, 1.0, 0.0))
    blk_lane = lax.shift_right_logical(t0 + lax.broadcasted_iota(jnp.int32, (1, TQ), 1), int(math.log2(SEL_BLOCK)))
    sel_t = jnp.where((rank < n_top) & (s_row <= blk_lane), 1.0, 0.0)
    if n_sel < LANES:
        sel_t = jnp.concatenate([sel_t, jnp.zeros((LANES - n_sel, TQ), F32)], axis=0)
    sel = sel_t.T.astype(BF16)
    selx_sc[...] = (_dot(sel, ex_ref[...]) - 1.0) * (-NEG_INF)
    near_off = pl.multiple_of(jnp.maximum(qi - 1, 0) * TQ, TQ)
    near_sel = selx_sc[:, pl.ds(near_off, 2 * TQ)]
    selx_sc[:, pl.ds(near_off, 2 * TQ)] = jnp.full((TQ, 2 * TQ), NEG_INF, F32)

    def init():
        m_sc[...] = jnp.full(m_sc.shape, NEG_INF, F32)
        l_sc[...] = jnp.zeros(l_sc.shape, F32)
        acc_sc[...] = jnp.zeros(acc_sc.shape, F32)

    def tiles(kinds):
        return jnp.concatenate([bt_ref[0, k] for k in kinds], axis=1)

    def step(k_ref, v_ref, start, bias, selmask):
        width = bias.shape[1]
        off = pl.multiple_of(start, TQ)
        kt = k_ref[0, 0, 0, pl.ds(off, width), :]
        vt = v_ref[0, 0, 0, pl.ds(off, width), :]
        s = _dot_nt(q4, kt) + bias
        if selmask is not None:
            s = s + jnp.concatenate([selmask] * NSA_GROUP, axis=0)
        m_prev = m_sc[...]
        m_next = jnp.maximum(m_prev, jnp.max(s, axis=1, keepdims=True))
        alpha = jnp.exp(m_prev - m_next)
        p = jnp.exp(s - jnp.concatenate([m_next] * (width // LANES), axis=1))
        l_sc[...] = alpha * l_sc[...] + jnp.sum(p, axis=1, keepdims=True)
        acc_sc[...] = acc_sc[...] * alpha[:, 0:HEAD_DIM] + _dot(p.astype(BF16), vt)
        m_sc[...] = m_next

    def finish():
        return acc_sc[...] / l_sc[...][:, 0:HEAD_DIM]

    near_kinds = (BT_PREV, BT_DIAG)

    init()

    def far_body(c, carry):
        off = pl.multiple_of(c * FAR_CHUNK, FAR_CHUNK)
        step(ks_ref, vs_ref, off, tiles((BT_FAR,) * (FAR_CHUNK // TQ)), selx_sc[:, pl.ds(off, FAR_CHUNK)])
        return carry

    far_tiles = jnp.maximum(qi - 1, 0)
    lax.fori_loop(0, (far_tiles + FAR_CHUNK // TQ - 1) // (FAR_CHUNK // TQ), far_body, 0)

    @pl.when(qi >= 1)
    def _():
        step(ks_ref, vs_ref, (qi - 1) * TQ, tiles(near_kinds), near_sel)

    @pl.when(qi == 0)
    def _():
        step(ks_ref, vs_ref, 0, tiles((BT_DIAG,)), near_sel[:, 0:TQ])

    o_sel = finish()

    init()
    n_win = WINDOW // TQ

    @pl.when(qi >= n_win)
    def _():
        step(kw_ref, vw_ref, (qi - n_win) * TQ, tiles((BT_WEDGE,) + (BT_FAR,) * (n_win - 2)), None)
        step(kw_ref, vw_ref, (qi - 1) * TQ, tiles(near_kinds), None)

    @pl.when(qi < n_win)
    def _():
        for d in range(n_win - 1, 0, -1):
            @pl.when(qi >= d)
            def _(d=d):
                step(kw_ref, vw_ref, (qi - d) * TQ, tiles((BT_PREV if d == 1 else BT_FAR,)), None)
        step(kw_ref, vw_ref, qi * TQ, tiles((BT_DIAG,)), None)

    o_win = finish()

    gates = jax.nn.sigmoid(gn_ref[...].astype(F32))
    outs = []
    for r in range(NSA_GROUP):
        sl = slice(r * TQ, (r + 1) * TQ)
        o_r = (gates[:, r:r + 1] * o_cmp[sl]
               + gates[:, NSA_GROUP + r:NSA_GROUP + r + 1] * o_sel[sl]
               + gates[:, 2 * NSA_GROUP + r:2 * NSA_GROUP + r + 1] * o_win[sl])
        outs.append(o_r)
    o_ref[...] = jnp.concatenate(outs, axis=1).astype(o_ref.dtype)


def _nsa(z, kvc, kv6, btiles, batch, seq):
    nq = seq // TQ
    n = batch * seq
    ov, expand = _nsa_constants(seq)
    nchp = kvc[0].shape[2]
    rows = NSA_GROUP * TQ
    qblk = Q_OFF // KV_WIDTH
    gblk = GN_OFF // LANES

    def kvspec(k):
        return pl.BlockSpec((1, 1, 1, seq, HEAD_DIM), lambda b, g, i, k=k: (k, b, g, 0, 0))

    cspec = pl.BlockSpec((1, 1, nchp, HEAD_DIM), lambda b, g, i: (b, g, 0, 0))
    return pl.pallas_call(
        functools.partial(_nsa_kernel, seq=seq),
        grid=(batch, NSA_KV_HEADS, nq),
        in_specs=[
            pl.BlockSpec((TQ, KV_WIDTH), lambda b, g, i: (b * nq + i, qblk + g)),
            pl.BlockSpec((TQ, LANES), lambda b, g, i: (b * nq + i, gblk + g)),
            cspec, cspec,
            kvspec(2), kvspec(3), kvspec(4), kvspec(5),
            pl.BlockSpec((1, N_BT, rows, LANES), lambda b, g, i: (g, 0, 0, 0)),
            pl.BlockSpec(ov.shape, lambda b, g, i: (0, 0)),
            pl.BlockSpec(expand.shape, lambda b, g, i: (0, 0)),
        ],
        out_specs=pl.BlockSpec((TQ, KV_WIDTH), lambda b, g, i: (b * nq + i, g)),
        out_shape=jax.ShapeDtypeStruct((n, NSA_WIDTH), BF16),
        scratch_shapes=[
            pltpu.VMEM((rows, LANES), F32),
            pltpu.VMEM((rows, LANES), F32),
            pltpu.VMEM((rows, HEAD_DIM), F32),
            pltpu.VMEM((TQ, seq), F32),
        ],
        compiler_params=_cparams(("arbitrary", "arbitrary", "arbitrary")),
        name="nsa",
    )(z, z, kvc[0], kvc[1], kv6, kv6, kv6, kv6, btiles, jnp.asarray(ov, BF16), jnp.asarray(expand, BF16))


def _memkv_kernel(mem_ref, g_ref, w_ref, k_ref, v_ref):
    hn = _rms(mem_ref[0], g_ref[...]).astype(BF16)
    kv = _dot(hn, w_ref[...])
    k_ref[0] = kv[:, 0:X_WIDTH].astype(BF16)
    v_ref[0] = kv[:, X_WIDTH:2 * X_WIDTH].astype(BF16)


def _memkv(mem, g_mem, w_mem_kv):
    b, m, d = mem.shape
    out = jax.ShapeDtypeStruct((b, m, X_WIDTH), BF16)
    ospec = pl.BlockSpec((1, m, X_WIDTH), lambda i: (i, 0, 0))
    return pl.pallas_call(
        _memkv_kernel,
        grid=(b,),
        in_specs=[
            pl.BlockSpec((1, m, d), lambda i: (i, 0, 0)),
            pl.BlockSpec((1, d), lambda i: (0, 0)),
            pl.BlockSpec((d, 2 * X_WIDTH), lambda i: (0, 0)),
        ],
        out_specs=[ospec, ospec],
        out_shape=[out, out],
        compiler_params=_cparams(("arbitrary",)),
        name="memkv",
    )(mem, g_mem, w_mem_kv)


def _merge_kernel(gm_ref, pool_ref, prev_ref, qx_ref, onsa_ref, x_ref, km_ref, vm_ref,
                  wgrp_ref, pscale_ref, wpo_ref, wno_ref, wxo_ref, wo_ref, gffn_ref, wpq_ref, sk_ref,
                  x1_ref, xb_ref, st_ref, *, seq):
    i = pl.program_id(0)
    tm = x_ref.shape[0]
    tpos = lax.rem(i * tm, seq) + lax.broadcasted_iota(jnp.int32, (tm, 1), 0)

    u_cur = pool_ref[...].astype(F32)
    u_prev = jnp.where(lax.rem(i * tm, seq) == 0, 0.0, prev_ref[...].astype(F32))
    ext = jnp.concatenate([u_prev, u_cur], axis=0)
    mixed = []
    for gi, w in enumerate(POOL_WINDOWS):
        cs = slice(gi * POOL_GROUP_DIM, (gi + 1) * POOL_GROUP_DIM)
        e = ext[:, cs]
        tot = e[MAX_POOL_WINDOW:MAX_POOL_WINDOW + tm]
        for k in range(1, w):
            tot = tot + e[MAX_POOL_WINDOW - k:MAX_POOL_WINDOW - k + tm]
        cnt = jnp.minimum(tpos + 1, w).astype(F32)
        pooled = tot / cnt - u_cur[:, cs]
        mixed.append(_dot(pooled.astype(BF16), wgrp_ref[gi]))
    mixed = jnp.concatenate(mixed, axis=1) * pscale_ref[...]
    y_pool = _dot(mixed.astype(BF16), wpo_ref[...])

    qx = qx_ref[...]
    km = km_ref[0]
    vm = vm_ref[0]
    o_mem = []
    for h in range(X_HEADS):
        hs = slice(h * X_HEAD_DIM, (h + 1) * X_HEAD_DIM)
        s = _dot_nt(qx[:, hs], km[:, hs]) * (X_HEAD_DIM ** -0.5)
        p = jnp.exp(s - jnp.max(s, axis=1, keepdims=True))
        o_mem.append(_dot(p.astype(BF16), vm[:, hs]) / jnp.sum(p, axis=1, keepdims=True))
    y_mem = _dot(jnp.concatenate(o_mem, axis=1).astype(BF16), wxo_ref[...])

    y_nsa = _dot(onsa_ref[...], wno_ref[...])

    gm = jax.nn.sigmoid(gm_ref[...].astype(F32))
    merged = gm[:, 0:D_MODEL] * y_pool + gm[:, D_MODEL:2 * D_MODEL] * y_nsa + gm[:, 2 * D_MODEL:] * y_mem
    x1 = x_ref[...] + _dot(merged.astype(BF16), wo_ref[...])
    x1_ref[...] = x1

    xb = _rms(x1, gffn_ref[...]).astype(BF16)
    xb_ref[...] = xb
    qp = _dot(xb, wpq_ref[...]).astype(BF16)
    for hp in range(2 * PEER_HEADS):
        st_ref[hp] = _dot_nt(sk_ref[hp], qp[:, hp * PEER_HALF:(hp + 1) * PEER_HALF])


def _merge(z, o_nsa, x2d, km, vm, w, seq, tm=256):
    n = x2d.shape[0]
    nkeys = w["sk"].shape[1]
    pblk = POOL_OFF // POOL_WIDTH
    prev_per_tile = tm // MAX_POOL_WINDOW

    def const(a):
        nd = a.ndim
        return pl.BlockSpec(a.shape, lambda i, nd=nd: (0,) * nd)

    weights = [w["wgrp"], w["pscale"], w["wpo"], w["wno"], w["wxo"], w["wo"], w["gffn"], w["wpq"], w["sk"]]
    return pl.pallas_call(
        functools.partial(_merge_kernel, seq=seq),
        grid=(n // tm,),
        in_specs=[
            pl.BlockSpec((tm, GM_W), lambda i: (i, 0)),
            pl.BlockSpec((tm, POOL_WIDTH), lambda i: (i, pblk)),
            pl.BlockSpec((MAX_POOL_WINDOW, POOL_WIDTH),
                         lambda i: (jnp.maximum(i * prev_per_tile - 1, 0), pblk)),
            pl.BlockSpec((tm, X_WIDTH), lambda i: (i, QX_OFF // X_WIDTH)),
            pl.BlockSpec((tm, NSA_WIDTH), lambda i: (i, 0)),
            pl.BlockSpec((tm, D_MODEL), lambda i: (i, 0)),
            pl.BlockSpec((1,) + km.shape[1:], lambda i: ((i * tm) // seq, 0, 0)),
            pl.BlockSpec((1,) + vm.shape[1:], lambda i: ((i * tm) // seq, 0, 0)),
        ] + [const(a) for a in weights],
        out_specs=[
            pl.BlockSpec((tm, D_MODEL), lambda i: (i, 0)),
            pl.BlockSpec((tm, D_MODEL), lambda i: (i, 0)),
            pl.BlockSpec((2 * PEER_HEADS, nkeys, tm), lambda i: (0, 0, i)),
        ],
        out_shape=[
            jax.ShapeDtypeStruct((n, D_MODEL), F32),
            jax.ShapeDtypeStruct((n, D_MODEL), BF16),
            jax.ShapeDtypeStruct((2 * PEER_HEADS, nkeys, n), F32),
        ],
        compiler_params=_cparams(("arbitrary",)),
        name="merge",
    )(z, z, z, z, o_nsa, x2d, km, vm, *weights)


def _top_desc(s, count):
    rows = s.shape[0]
    ridx = lax.broadcasted_iota(jnp.int32, s.shape, 0)
    out = []
    for _ in range(count):
        m = jnp.max(s, axis=0, keepdims=True)
        out.append(m)
        first = jnp.min(jnp.where(s == m, ridx, rows), axis=0, keepdims=True)
        s = jnp.where(ridx == first, -jnp.inf, s)
    return out


def _route_kernel(st_ref, rt_ref):
    tn = st_ref.shape[2]
    taus, m1s, m2s, zinvs = [], [], [], []
    for h in range(PEER_HEADS):
        a = _top_desc(st_ref[2 * h], PEER_TOPK)
        b = _top_desc(st_ref[2 * h + 1], PEER_TOPK)
        cand = [a[i] + b[j] for i in range(PEER_TOPK) for j in range(PEER_TOPK) if (i + 1) * (j + 1) <= PEER_TOPK]
        pad = (-len(cand)) % 8
        cand = jnp.concatenate(cand + [jnp.full((pad, tn), -jnp.inf, F32)], axis=0)
        best = _top_desc(cand, PEER_TOPK)
        z = jnp.zeros((1, tn), F32)
        for bs in best:
            z = z + jnp.exp(bs - best[0])
        taus.append(best[-1]); m1s.append(a[0]); m2s.append(b[0]); zinvs.append(1.0 / z)
    for k, vals in enumerate((taus, m1s, m2s, zinvs)):
        rt_ref[k] = jnp.concatenate(vals, axis=0)


def _route(st, tn=256):
    hp, nkeys, n = st.shape
    return pl.pallas_call(
        _route_kernel,
        grid=(n // tn,),
        in_specs=[pl.BlockSpec((hp, nkeys, tn), lambda i: (0, 0, i))],
        out_specs=pl.BlockSpec((4, PEER_HEADS, tn), lambda i: (0, 0, i)),
        out_shape=jax.ShapeDtypeStruct((4, PEER_HEADS, n), F32),
        compiler_params=_cparams(("arbitrary",)),
        name="route",
    )(st)


def _peer_kernel(xb_ref, st_ref, rt_ref, u_ref, vt_ref, x1_ref, gfin_ref, y_ref, acc_sc, e1_sc, e2_sc, *, k1_per_chunk):
    c = pl.program_id(1)
    nkeys = st_ref.shape[1]

    @pl.when(c == 0)
    def _():
        acc_sc[...] = jnp.zeros(acc_sc.shape, F32)
        for h in range(PEER_HEADS):
            e1_sc[h] = jnp.exp(st_ref[2 * h] - rt_ref[1, h:h + 1, :]) * rt_ref[3, h:h + 1, :]
            e2_sc[h] = jnp.exp(st_ref[2 * h + 1] - rt_ref[2, h:h + 1, :])

    act = _dot_nt(u_ref[...], xb_ref[...])
    parts = []
    for k in range(k1_per_chunk):
        k1 = c * k1_per_chunk + k
        wsum = None
        for h in range(PEER_HEADS):
            s1 = st_ref[2 * h, pl.ds(k1, 1), :]
            e1 = e1_sc[h, pl.ds(k1, 1), :]
            term = jnp.where(s1 + st_ref[2 * h + 1] >= rt_ref[0, h:h + 1, :], e2_sc[h], 0.0) * e1
            wsum = term if wsum is None else wsum + term
        a = act[k * nkeys:(k + 1) * nkeys]
        gelu = 0.5 * a * (1.0 + lax.erf(a * (2.0 ** -0.5)))
        parts.append((gelu * wsum).astype(BF16))
    acc_sc[...] += _dot(vt_ref[...], jnp.concatenate(parts, axis=0))

    @pl.when(c == pl.num_programs(1) - 1)
    def _():
        x2 = x1_ref[...] + acc_sc[...].T
        y_ref[...] = _rms(x2, gfin_ref[...])


def _peer(xb, st, rt, u_bf, vt_bf, x1, g_final, tn=256, k1_per_chunk=8):
    n = xb.shape[0]
    hp, nkeys, _ = st.shape
    ne = u_bf.shape[0]
    ce = k1_per_chunk * nkeys
    return pl.pallas_call(
        functools.partial(_peer_kernel, k1_per_chunk=k1_per_chunk),
        grid=(n // tn, ne // ce),
        in_specs=[
            pl.BlockSpec((tn, D_MODEL), lambda i, c: (i, 0)),
            pl.BlockSpec((hp, nkeys, tn), lambda i, c: (0, 0, i)),
            pl.BlockSpec((4, PEER_HEADS, tn), lambda i, c: (0, 0, i)),
            pl.BlockSpec((ce, D_MODEL), lambda i, c: (c, 0)),
            pl.BlockSpec((D_MODEL, ce), lambda i, c: (0, c)),
            pl.BlockSpec((tn, D_MODEL), lambda i, c: (i, 0)),
            pl.BlockSpec((1, D_MODEL), lambda i, c: (0, 0)),
        ],
        out_specs=pl.BlockSpec((tn, D_MODEL), lambda i, c: (i, 0)),
        out_shape=jax.ShapeDtypeStruct((n, D_MODEL), F32),
        scratch_shapes=[
            pltpu.VMEM((D_MODEL, tn), F32),
            pltpu.VMEM((PEER_HEADS, nkeys, tn), F32),
            pltpu.VMEM((PEER_HEADS, nkeys, tn), F32),
        ],
        compiler_params=_cparams(("arbitrary", "arbitrary")),
        name="peer",
    )(xb, st, rt, u_bf, vt_bf, x1, g_final)


def _layer(x, mem, rel_bias, g_mix, w_in, w_pool_grp, pool_scale, w_pool_out, w_cmp_k, w_cmp_v, pe_k, pe_v,
           w_nsa_out, g_mem, w_mem_kv, w_x_out, w_o, g_ffn, w_peer_q, peer_sub_keys, peer_u, peer_v):
    batch, seq, d = x.shape
    n = batch * seq
    x2d = x.reshape(n, d)

    z = _inproj(x2d, g_mix.reshape(1, d), _relayout_w_in(w_in))

    kv6 = z[:, KV_OFF:KV_OFF + 6 * KV_WIDTH].reshape(batch, seq, 6, NSA_KV_HEADS, HEAD_DIM).transpose(2, 0, 3, 1, 4)
    nch = seq // CMP_STRIDE
    chunk_w = CMP_STRIDE * HEAD_DIM
    kvc = _compress(kv6[0].reshape(batch, NSA_KV_HEADS, nch, chunk_w), kv6[1].reshape(batch, NSA_KV_HEADS, nch, chunk_w),
                    pe_k.reshape(2, chunk_w), pe_v.reshape(2, chunk_w), w_cmp_k.astype(BF16), w_cmp_v.astype(BF16))
    o_nsa = _nsa(z, kvc, kv6, _biasprep(rel_bias), batch, seq)

    km, vm = _memkv(mem, g_mem.reshape(1, d), w_mem_kv.astype(BF16))

    nkeys = peer_sub_keys.shape[2]
    weights = dict(
        wgrp=w_pool_grp.astype(BF16), pscale=pool_scale.reshape(1, POOL_WIDTH), wpo=w_pool_out.astype(BF16),
        wno=w_nsa_out.astype(BF16), wxo=w_x_out.astype(BF16), wo=w_o.astype(BF16), gffn=g_ffn.reshape(1, d),
        wpq=w_peer_q.astype(BF16), sk=peer_sub_keys.reshape(2 * PEER_HEADS, nkeys, PEER_HALF).astype(BF16))
    x1, xb, st = _merge(z, o_nsa, x2d, km, vm, weights, seq)
    rt = _route(st)
    return x1, xb, st, rt, peer_u.astype(BF16), peer_v.T.astype(BF16)


def kernel(x, mem, rel_bias, g_mix, w_in, w_pool_grp, pool_scale, w_pool_out, w_cmp_k, w_cmp_v, pe_k, pe_v, w_nsa_out, g_mem, w_mem_kv, w_x_out, w_o, g_ffn, w_peer_q, peer_sub_keys, peer_u, peer_v, g_final):
    assert g_mix.shape[0] == 1, "single layer"
    x1, xb, st, rt, u_bf, vt_bf = _layer(
        x, mem, rel_bias, g_mix[0], w_in[0], w_pool_grp[0], pool_scale[0], w_pool_out[0], w_cmp_k[0], w_cmp_v[0],
        pe_k[0], pe_v[0], w_nsa_out[0], g_mem[0], w_mem_kv[0], w_x_out[0], w_o[0], g_ffn[0], w_peer_q[0],
        peer_sub_keys[0], peer_u[0], peer_v[0])
    y = _peer(xb, st, rt, u_bf, vt_bf, x1, g_final.reshape(1, -1))
    return y.reshape(x.shape)
```
